```python
import jax, jax.numpy as jnp
from jax import lax
import numpy as np

D_MODEL = 1024
BATCH = 8
SEQ = 4096
DEPTH = 1

D_MIX = 2 * D_MODEL
D_SSD = D_MIX // 2
D_SC = D_MIX - D_SSD
SSD_HEADDIM = 64
SSD_HEADS = D_SSD // SSD_HEADDIM
SSD_GROUPS = 4
SSD_HPG = SSD_HEADS // SSD_GROUPS
SSD_STATE = 128
SSD_CONV = 4
D_XBC = D_SSD + 2 * SSD_GROUPS * SSD_STATE
CHUNK = 128
SC_GROUPS = 16
SC_CONV = 3
N_META = 16
META_PAD = CHUNK - N_META
IN_COLS = D_SSD + D_XBC + SSD_HEADS + 3 * D_SC
N_EXPERT_GROUPS = 4
EXPERTS_PER_GROUP = 4
N_EXPERTS = N_EXPERT_GROUPS * EXPERTS_PER_GROUP
TOP_K = 2
D_EXPERT = 512
EPS = 1e-6

kernel_name = "hymba_ssd_shortconv_hier_moe_block"


def rmsnorm(x, g):
    xf = x.astype(jnp.float32)
    xf = xf * lax.rsqrt(jnp.mean(xf * xf, axis=-1, keepdims=True) + EPS)
    return (xf * g.astype(jnp.float32)).astype(x.dtype)


def group_rmsnorm(x, g, n_groups):
    shp = x.shape
    xf = x.astype(jnp.float32).reshape(shp[:-1] + (n_groups, shp[-1] // n_groups))
    xf = xf * lax.rsqrt(jnp.mean(xf * xf, axis=-1, keepdims=True) + EPS)
    return (xf.reshape(shp) * g.astype(jnp.float32)).astype(x.dtype)


def causal_dwconv(x, w):
    K = w.shape[0]
    L = x.shape[1]
    xp = jnp.pad(x, ((0, 0), (K - 1, 0), (0, 0)))
    y = xp[:, 0:L] * w[0]
    for k in range(1, K):
        y = y + xp[:, k:k + L] * w[k]
    return y


def ssd_chunked(xs, dt, A, Bm, Cm):
    dtype = xs.dtype
    b, Lp = xs.shape[0], xs.shape[1]
    nc = Lp // CHUNK
    x = xs.reshape(b, nc, CHUNK, SSD_GROUPS, SSD_HPG, SSD_HEADDIM)
    dtc = dt.reshape(b, nc, CHUNK, SSD_GROUPS, SSD_HPG)
    Bc = Bm.reshape(b, nc, CHUNK, SSD_GROUPS, SSD_STATE)
    Cc = Cm.reshape(b, nc, CHUNK, SSD_GROUPS, SSD_STATE)

    dA = jnp.moveaxis(dtc * A.reshape(SSD_GROUPS, SSD_HPG), 2, -1)
    A_cum = jnp.cumsum(dA, axis=-1)
    xdt = x * dtc.astype(dtype)[..., None]

    causal = jnp.tril(jnp.ones((CHUNK, CHUNK), dtype=bool))
    seg = A_cum[..., :, None] - A_cum[..., None, :]
    Lmat = jnp.exp(jnp.where(causal, seg, -jnp.inf)).astype(dtype)
    CB = jnp.einsum('bcqgn,bckgn->bcgqk', Cc, Bc)
    y_diag = jnp.einsum('bcgqk,bcgrqk,bckgrp->bcqgrp', CB, Lmat, xdt)

    decay_states = jnp.exp(A_cum[..., -1:] - A_cum).astype(dtype)
    states = jnp.einsum('bckgn,bcgrk,bckgrp->bcgrpn', Bc, decay_states, xdt)
    chunk_decay = jnp.exp(A_cum[..., -1]).astype(dtype)

    def step(s, inp):
        st, dec = inp
        return dec[..., None, None] * s + st, s

    s0 = jnp.zeros((b, SSD_GROUPS, SSD_HPG, SSD_HEADDIM, SSD_STATE), dtype)
    _, prev = lax.scan(step, s0, (jnp.moveaxis(states, 1, 0), jnp.moveaxis(chunk_decay, 1, 0)))
    prev = jnp.moveaxis(prev, 0, 1)

    decay_out = jnp.exp(A_cum).astype(dtype)
    y_off = jnp.einsum('bcqgn,bcgrpn,bcgrq->bcqgrp', Cc, prev, decay_out)
    return (y_diag + y_off).reshape(b, Lp, SSD_HEADS, SSD_HEADDIM)


def ssd_branch(z, xbc, dt_raw, conv_w, conv_b, dt_bias, A_log, D_skip, norm_g):
    b, L, _ = xbc.shape
    xbc = jax.nn.silu(causal_dwconv(xbc, conv_w) + conv_b)
    xs = xbc[..., :D_SSD].reshape(b, L, SSD_HEADS, SSD_HEADDIM)
    Bm = xbc[..., D_SSD:D_SSD + SSD_GROUPS * SSD_STATE].reshape(b, L, SSD_GROUPS, SSD_STATE)
    Cm = xbc[..., D_SSD + SSD_GROUPS * SSD_STATE:].reshape(b, L, SSD_GROUPS, SSD_STATE)
    dt = jax.nn.softplus((dt_raw + dt_bias).astype(jnp.float32))
    A = -jnp.exp(A_log.astype(jnp.float32))

    def front_pad(t):
        return jnp.pad(t, ((0, 0), (META_PAD, 0)) + ((0, 0),) * (t.ndim - 2))

    y = ssd_chunked(front_pad(xs), front_pad(dt), A, front_pad(Bm), front_pad(Cm))[:, META_PAD:]
    y = y + D_skip[:, None] * xs
    y = y.reshape(b, L, D_SSD)
    return group_rmsnorm(y * jax.nn.silu(z), norm_g, SSD_GROUPS)


def short_conv_branch(gate_b, gate_c, h, conv_w, norm_g):
    v = causal_dwconv(gate_c * h, conv_w)
    return group_rmsnorm(gate_b * v, norm_g, SC_GROUPS)


def hier_moe(u, wg_r, bg_r, we_r, be_r, w_gate, w_up, w_down):
    b, L, d = u.shape
    t = u.reshape(b * L, d)
    p_group = jax.nn.softmax((t @ wg_r + bg_r).astype(jnp.float32), axis=-1)
    g_val, g_idx = lax.top_k(p_group, 1)
    e_logits = (t @ we_r + be_r).astype(jnp.float32).reshape(-1, N_EXPERT_GROUPS, EXPERTS_PER_GROUP)
    sel = jnp.take_along_axis(e_logits, g_idx[:, :, None], axis=1)[:, 0]
    top_v, top_i = lax.top_k(sel, TOP_K)
    top_w = jax.nn.softmax(top_v, axis=-1)
    in_w = jnp.einsum('tk,tke->te', top_w, jax.nn.one_hot(top_i, EXPERTS_PER_GROUP, dtype=jnp.float32))
    grp_w = jax.nn.one_hot(g_idx[:, 0], N_EXPERT_GROUPS, dtype=jnp.float32) * g_val
    combine = (grp_w[:, :, None] * in_w[:, None, :]).reshape(-1, N_EXPERTS).astype(u.dtype)
    out = jnp.zeros_like(t)
    for e in range(N_EXPERTS):
        h = jax.nn.silu(t @ w_gate[e]) * (t @ w_up[e])
        out = out + combine[:, e:e + 1] * (h @ w_down[e])
    return out.reshape(b, L, d)


def setup_inputs(seed: int = 0) -> dict:
    key = jax.random.key(seed)
    ks = jax.random.split(key, 24)
    f32 = jnp.float32
    nrm = lambda k, shp, s: (jax.random.normal(k, shp, f32) * s)
    dt0 = jnp.exp(jax.random.uniform(ks[6], (DEPTH, SSD_HEADS), f32, np.log(1e-3), np.log(1e-1)))
    return {
        "x": jax.random.normal(ks[0], (BATCH, SEQ, D_MODEL), f32),
        "meta_tokens": nrm(ks[1], (N_META, D_MODEL), 1.0),
        "norm_mix": 1.0 + nrm(ks[2], (DEPTH, D_MODEL), 0.02),
        "w_in": nrm(ks[3], (DEPTH, D_MODEL, IN_COLS), D_MODEL ** -0.5),
        "ssd_conv_w": nrm(ks[4], (DEPTH, SSD_CONV, D_XBC), SSD_CONV ** -0.5),
        "ssd_conv_b": nrm(ks[5], (DEPTH, D_XBC), 0.02),
        "dt_bias": dt0 + jnp.log(-jnp.expm1(-dt0)),
        "A_log": jnp.log(jax.random.uniform(ks[7], (DEPTH, SSD_HEADS), f32, 1.0, 16.0)),
        "D_skip": 1.0 + nrm(ks[8], (DEPTH, SSD_HEADS), 0.1),
        "ssd_norm": 1.0 + nrm(ks[9], (DEPTH, D_SSD), 0.02),
        "sc_conv_w": nrm(ks[10], (DEPTH, SC_CONV, D_SC), SC_CONV ** -0.5),
        "sc_norm": 1.0 + nrm(ks[11], (DEPTH, D_SC), 0.02),
        "w_out": nrm(ks[12], (DEPTH, D_MIX, D_MODEL), D_MIX ** -0.5),
        "norm_ffn": 1.0 + nrm(ks[13], (DEPTH, D_MODEL), 0.02),
        "router_group_w": nrm(ks[14], (DEPTH, D_MODEL, N_EXPERT_GROUPS), D_MODEL ** -0.5),
        "router_group_b": nrm(ks[15], (DEPTH, N_EXPERT_GROUPS), 0.01),
        "router_expert_w": nrm(ks[16], (DEPTH, D_MODEL, N_EXPERTS), D_MODEL ** -0.5),
        "router_expert_b": nrm(ks[17], (DEPTH, N_EXPERTS), 0.01),
        "expert_w_gate": nrm(ks[18], (DEPTH, N_EXPERTS, D_MODEL, D_EXPERT), D_MODEL ** -0.5),
        "expert_w_up": nrm(ks[19], (DEPTH, N_EXPERTS, D_MODEL, D_EXPERT), D_MODEL ** -0.5),
        "expert_w_down": nrm(ks[20], (DEPTH, N_EXPERTS, D_EXPERT, D_MODEL), D_EXPERT ** -0.5),
        "norm_final": 1.0 + nrm(ks[21], (D_MODEL,), 0.02),
    }


def reference(x, meta_tokens, norm_mix, w_in, ssd_conv_w, ssd_conv_b, dt_bias, A_log, D_skip,
              ssd_norm, sc_conv_w, sc_norm, w_out, norm_ffn, router_group_w, router_group_b,
              router_expert_w, router_expert_b, expert_w_gate, expert_w_up, expert_w_down,
              norm_final):
    b = x.shape[0]
    meta = jnp.broadcast_to(meta_tokens.astype(x.dtype)[None], (b, N_META, D_MODEL))
    h = jnp.concatenate([meta, x], axis=1)
    o_z = D_SSD
    o_xbc = o_z + D_XBC
    o_dt = o_xbc + SSD_HEADS
    o_b = o_dt + D_SC
    o_c = o_b + D_SC
    for l in range(DEPTH):
        u = rmsnorm(h, norm_mix[l])
        proj = u @ w_in[l]
        y_ssd = ssd_branch(proj[..., :o_z], proj[..., o_z:o_xbc], proj[..., o_xbc:o_dt],
                           ssd_conv_w[l], ssd_conv_b[l], dt_bias[l], A_log[l], D_skip[l], ssd_norm[l])
        y_sc = short_conv_branch(proj[..., o_dt:o_b], proj[..., o_b:o_c], proj[..., o_c:],
                                 sc_conv_w[l], sc_norm[l])
        h = h + jnp.concatenate([y_ssd, y_sc], axis=-1) @ w_out[l]
        u2 = rmsnorm(h, norm_ffn[l])
        h = h + hier_moe(u2, router_group_w[l], router_group_b[l], router_expert_w[l],
                         router_expert_b[l], expert_w_gate[l], expert_w_up[l], expert_w_down[l])
    return rmsnorm(h, norm_final)[:, N_META:]
```

```python
import functools

import jax
import jax.numpy as jnp
from jax import lax
from jax.experimental import pallas as pl
from jax.experimental.pallas import tpu as pltpu

F32 = jnp.float32
BF16 = jnp.bfloat16

D_MODEL = 1024
D_SSD = 1024
D_SC = 1024
HEADDIM = 64
HEADS = 16
GROUPS = 4
GROUP_W = D_SSD // GROUPS
STATE = 128
SSD_CONV = 4
SC_CONV = 3
SC_GROUP_W = 64
CHUNK = 128
N_META = 16
N_EXPERT_GROUPS = 4
EXPERTS_PER_GROUP = 4
N_EXPERTS = 16
D_EXPERT = 512
EPS = 1e-6
LANES = 128
TAIL = 8
NEG = -1e30

O_Z = 0
O_XBC = 1024
O_SCB = 3072
O_SCC = 4096
O_SCH = 5120
W_MAIN_COLS = 6144

VMEM_LIMIT_BYTES = 56 * 1024 * 1024


def _silu(v):
    return v / (1.0 + jnp.exp(-v))


def _split_bf16(v):
    hi = v.astype(BF16)
    lo = (v - hi.astype(F32)).astype(BF16)
    return hi, lo


def _dot(a, b):
    return jnp.dot(a, b, preferred_element_type=F32)


def _dot3(a_hi, a_lo, b_hi_ref, b_lo_ref):
    b_hi = b_hi_ref[...]
    return _dot(a_hi, b_hi) + _dot(a_lo, b_hi) + _dot(a_hi, b_lo_ref[...])


def _mixer_kernel(x_ref, meta_ref, gmix_ref, wmain_ref, wdth_ref, wdtl_ref, convw_ref, convb_ref,
                  dtb_ref, alog_ref, dskip_ref, ssdn_ref, scw_ref, scn_ref, wout_ref, gffn_ref,
                  wrh_ref, wrl_ref, br_ref, tril_ref, e1_ref, e2_ref, ones_ref, bd64_ref,
                  h1_ref, u2_ref, comb_ref,
                  proj_ref, cbuf_ref, sbuf_ref, xs_ref, bm_ref, cm_ref, dt_ref, ycat_ref, st_ref,
                  *, tm):
    j = pl.program_id(1)
    jv = jnp.zeros((tm, 1), jnp.int32) + j
    is_meta = jv == 0

    @pl.when(j == 0)
    def _init():
        st_ref[...] = jnp.zeros(st_ref.shape, F32)
        cbuf_ref[0:TAIL, :] = jnp.zeros((TAIL, cbuf_ref.shape[1]), F32)
        sbuf_ref[0:TAIL, :] = jnp.zeros((TAIL, sbuf_ref.shape[1]), F32)

    xin = jnp.where(is_meta, meta_ref[...], x_ref[0])
    ms = jnp.mean(xin * xin, axis=-1, keepdims=True)
    u = xin * lax.rsqrt(ms + EPS) * gmix_ref[...]
    u_hi, u_lo = _split_bf16(u)

    proj_ref[:, O_Z:O_XBC] = _dot(u_hi, wmain_ref[:, O_Z:O_XBC])
    cbuf_ref[TAIL:TAIL + tm, 0:1024] = _dot(u_hi, wmain_ref[:, O_XBC:O_XBC + 1024])
    cbuf_ref[TAIL:TAIL + tm, 1024:2048] = _dot(u_hi, wmain_ref[:, O_XBC + 1024:O_SCB])
    proj_ref[:, O_SCB:O_SCC] = _dot(u_hi, wmain_ref[:, O_SCB:O_SCC])
    proj_ref[:, O_SCC:O_SCH] = _dot(u_hi, wmain_ref[:, O_SCC:O_SCH])
    proj_ref[:, O_SCH:W_MAIN_COLS] = _dot(u_hi, wmain_ref[:, O_SCH:W_MAIN_COLS])

    dt_in = _dot3(u_hi, u_lo, wdth_ref, wdtl_ref) + dtb_ref[...]
    dt = jnp.maximum(dt_in, 0.0) + jnp.log1p(jnp.exp(-jnp.abs(dt_in)))
    row = lax.broadcasted_iota(jnp.int32, (tm, 1), 0)
    valid = jnp.logical_or(jv > 0, row >= tm - N_META)
    dt_ref[...] = jnp.where(valid, dt, 0.0)

    for s in range(8):
        c0 = s * 256
        acc = convb_ref[:, c0:c0 + 256] + convw_ref[0:1, c0:c0 + 256] * cbuf_ref[TAIL - 3:TAIL - 3 + tm, c0:c0 + 256]
        for k in range(1, SSD_CONV):
            acc = acc + convw_ref[k:k + 1, c0:c0 + 256] * cbuf_ref[TAIL - 3 + k:TAIL - 3 + k + tm, c0:c0 + 256]
        act = _silu(acc)
        if s < 4:
            xs_ref[:, c0:c0 + 256] = act
        elif s < 6:
            bm_ref[:, c0 - 1024:c0 - 768] = act
        else:
            cm_ref[:, c0 - 1536:c0 - 1280] = act
    cbuf_ref[0:TAIL, :] = cbuf_ref[tm:tm + TAIL, :]

    lane_row = lax.broadcasted_iota(jnp.int32, (1, LANES), 1)
    a_row = jnp.where(lane_row < HEADS, -jnp.exp(alog_ref[...]), 0.0)
    qi = lax.broadcasted_iota(jnp.int32, (CHUNK, CHUNK), 0)
    ki = lax.broadcasted_iota(jnp.int32, (CHUNK, CHUNK), 1)
    causal = qi >= ki
    left = ki < HEADDIM

    def chunk_body(c, carry):
        r0 = pl.multiple_of(c * CHUNK, CHUNK)
        rows = pl.ds(r0, CHUNK)
        dtc = dt_ref[rows, :]
        da_hi, da_lo = _split_bf16(dtc * a_row)
        acum = _dot(tril_ref[...], jnp.concatenate([da_hi, da_lo], axis=0))
        acum_t = acum.T
        ac_hi, ac_lo = _split_bf16(acum)
        ax2 = _dot(jnp.concatenate([ac_hi, ac_lo], axis=1), e2_ref[...])
        d_hi, d_lo = _split_bf16(dtc)
        dtx = _dot(jnp.concatenate([d_hi, d_lo], axis=1), e1_ref[...])

        for g in range(GROUPS):
            gl = slice(g * GROUP_W, (g + 1) * GROUP_W)
            ax1 = jnp.concatenate(
                [jnp.where(left, ax2[:, (4 * g + 2 * t) * LANES:(4 * g + 2 * t + 1) * LANES],
                           ax2[:, (4 * g + 2 * t + 1) * LANES:(4 * g + 2 * t + 2) * LANES]) for t in range(2)],
                axis=1)
            a_last = ax1[CHUNK - 1:CHUNK, :]
            xs_g = xs_ref[rows, gl]
            xdt = xs_g * dtx[:, gl]
            b_t = bm_ref[rows, g * STATE:(g + 1) * STATE].T
            b_t = b_t.astype(BF16)
            c_g = cm_ref[rows, g * STATE:(g + 1) * STATE].astype(BF16)
            cb = _dot(c_g, b_t)
            s_prev = st_ref[g]
            y_g = _dot(c_g, s_prev.astype(BF16)) * jnp.exp(ax1)
            tiles = []
            for t in range(2):
                xt = xdt[:, t * LANES:(t + 1) * LANES]
                m_parts = []
                for hh in range(2):
                    h = 4 * g + 2 * t + hh
                    seg = ax2[:, h * LANES:(h + 1) * LANES] - acum_t[h:h + 1, :]
                    lmat = jnp.exp(jnp.where(causal, seg, -jnp.inf))
                    m_parts.append((cb * lmat).astype(BF16))
                rhs = jnp.concatenate([jnp.where(left, xt, 0.0).astype(BF16),
                                       jnp.where(left, 0.0, xt).astype(BF16)], axis=0)
                tiles.append(_dot(jnp.concatenate(m_parts, axis=1), rhs))
            y_g = y_g + jnp.concatenate(tiles, axis=1)
            st_ref[g] = jnp.exp(a_last) * s_prev + _dot(b_t, (xdt * jnp.exp(a_last - ax1)).astype(BF16))

            y_g = y_g + dskip_ref[:, gl] * xs_g
            y_g = y_g * _silu(proj_ref[rows, O_Z + g * GROUP_W:O_Z + (g + 1) * GROUP_W])
            ssq = _dot((y_g * y_g).astype(BF16), ones_ref[...])
            y_g = y_g * lax.rsqrt(ssq * (1.0 / GROUP_W) + EPS) * ssdn_ref[:, gl]
            ycat_ref[rows, gl] = y_g.astype(BF16)
        return carry

    lax.fori_loop(0, tm // CHUNK, chunk_body, 0)

    for s in range(4):
        cs = slice(s * 256, (s + 1) * 256)
        sbuf_ref[TAIL:TAIL + tm, cs] = (proj_ref[:, O_SCC + s * 256:O_SCC + (s + 1) * 256]
                                        * proj_ref[:, O_SCH + s * 256:O_SCH + (s + 1) * 256])
        v = scw_ref[0:1, cs] * sbuf_ref[TAIL - 2:TAIL - 2 + tm, cs]
        for k in range(1, SC_CONV):
            v = v + scw_ref[k:k + 1, cs] * sbuf_ref[TAIL - 2 + k:TAIL - 2 + k + tm, cs]
        ysc = proj_ref[:, O_SCB + s * 256:O_SCB + (s + 1) * 256] * v
        ssq = _dot((ysc * ysc).astype(BF16), bd64_ref[...])
        ysc = ysc * lax.rsqrt(ssq * (1.0 / SC_GROUP_W) + EPS) * scn_ref[:, cs]
        ycat_ref[:, D_SSD + s * 256:D_SSD + (s + 1) * 256] = ysc.astype(BF16)
    sbuf_ref[0:TAIL, :] = sbuf_ref[tm:tm + TAIL, :]

    h1 = jnp.where(is_meta, meta_ref[...], x_ref[0]) + _dot(ycat_ref[...], wout_ref[...])
    h1_ref[0] = h1
    ms2 = jnp.mean(h1 * h1, axis=-1, keepdims=True)
    u2 = h1 * lax.rsqrt(ms2 + EPS) * gffn_ref[...]
    u2_ref[0] = u2.astype(BF16)
    v_hi, v_lo = _split_bf16(u2)
    lg = _dot3(v_hi, v_lo, wrh_ref, wrl_ref) + br_ref[...]

    lane = lax.broadcasted_iota(jnp.int32, (tm, LANES), 1)
    big = jnp.int32(4 * LANES)
    glog = jnp.where(lane < N_EXPERT_GROUPS, lg, NEG)
    gmax = jnp.max(glog, axis=-1, keepdims=True)
    gsum = jnp.sum(jnp.exp(glog - gmax), axis=-1, keepdims=True)
    g_val = 1.0 / gsum
    g_idx = jnp.min(jnp.where(glog == gmax, lane, big), axis=-1, keepdims=True)
    lo = N_EXPERT_GROUPS + EXPERTS_PER_GROUP * g_idx
    insel = jnp.logical_and(lane >= lo, lane < lo + EXPERTS_PER_GROUP)
    sel = jnp.where(insel, lg, NEG)
    v1 = jnp.max(sel, axis=-1, keepdims=True)
    i1 = jnp.min(jnp.where(jnp.logical_and(insel, sel == v1), lane, big), axis=-1, keepdims=True)
    rest = jnp.logical_and(insel, lane != i1)
    sel2 = jnp.where(rest, lg, NEG)
    v2 = jnp.max(sel2, axis=-1, keepdims=True)
    i2 = jnp.min(jnp.where(jnp.logical_and(rest, sel2 == v2), lane, big), axis=-1, keepdims=True)
    e21 = jnp.exp(v2 - v1)
    w1 = 1.0 / (1.0 + e21)
    w2 = e21 * w1
    comb = g_val * (jnp.where(lane == i1, w1, 0.0) + jnp.where(lane == i2, w2, 0.0))
    comb_ref[0] = comb


def _moe_kernel(h1_ref, u2_ref, comb_ref, wg_ref, wu_ref, wd_ref, gfin_ref, out_ref, *, tmo):
    e = pl.program_id(1)

    @pl.when(e == 0)
    def _init():
        out_ref[...] = h1_ref[...]

    u2 = u2_ref[...]
    hg = _dot(u2, wg_ref[0])
    hu = _dot(u2, wu_ref[0])
    lane = lax.broadcasted_iota(jnp.int32, (tmo, LANES), 1)
    cw = jnp.sum(jnp.where(lane == e + N_EXPERT_GROUPS, comb_ref[...], 0.0), axis=-1, keepdims=True)
    hmid = (_silu(hg) * hu * cw).astype(BF16)
    out_ref[...] += _dot(hmid, wd_ref[0])

    @pl.when(e == N_EXPERTS - 1)
    def _fin():
        acc = out_ref[...]
        ms = jnp.mean(acc * acc, axis=-1, keepdims=True)
        out_ref[...] = acc * lax.rsqrt(ms + EPS) * gfin_ref[...]


def _const_spec(shape):
    nd = len(shape)
    return pl.BlockSpec(shape, lambda *_: (0,) * nd)


def _pad_lanes(v, width=LANES):
    return jnp.pad(v, ((0, 0), (0, width - v.shape[1])))


def kernel(x, meta_tokens, norm_mix, w_in, ssd_conv_w, ssd_conv_b, dt_bias, A_log, D_skip, ssd_norm, sc_conv_w, sc_norm, w_out, norm_ffn, router_group_w, router_group_b, router_expert_w, router_expert_b, expert_w_gate, expert_w_up, expert_w_down, norm_final):
    assert w_in.shape[0] == 1, "single-layer block"
    b, seq, d = x.shape
    assert d == D_MODEL
    tm = 256 if seq % 256 == 0 else CHUNK
    assert seq % tm == 0
    nblk = seq // tm

    w = w_in[0]
    o_dt = D_SSD + 2048
    wmain = jnp.concatenate([w[:, :o_dt], w[:, o_dt + HEADS:]], axis=1).astype(BF16)
    wdt_hi, wdt_lo = _split_bf16(_pad_lanes(w[:, o_dt:o_dt + HEADS]))
    wr = _pad_lanes(jnp.concatenate([router_group_w[0], router_expert_w[0]], axis=1))
    wr_hi, wr_lo = _split_bf16(wr)
    br = _pad_lanes(jnp.concatenate([router_group_b[0], router_expert_b[0]])[None, :])
    row = lambda v: v.reshape(1, -1).astype(F32)
    meta_blk = jnp.concatenate([jnp.zeros((tm - N_META, d), F32), meta_tokens.astype(F32)], axis=0)

    r = jnp.arange(CHUNK)
    tril = (r[:, None] >= r[None, :]).astype(BF16)
    tril2 = jnp.concatenate([tril, tril], axis=1)
    hrow = jnp.arange(LANES)[:, None]
    e1 = (hrow == (jnp.arange(D_SSD)[None, :] // HEADDIM)).astype(BF16)
    e2 = (hrow == (jnp.arange(HEADS * LANES)[None, :] // LANES)).astype(BF16)
    e1 = jnp.concatenate([e1, e1], axis=0)
    e2 = jnp.concatenate([e2, e2], axis=0)
    ones = jnp.ones((GROUP_W, GROUP_W), BF16)
    cidx = jnp.arange(GROUP_W) // SC_GROUP_W
    bd64 = (cidx[:, None] == cidx[None, :]).astype(BF16)

    consts = [
        meta_blk, row(norm_mix[0]), wmain, wdt_hi, wdt_lo, ssd_conv_w[0].astype(F32), row(ssd_conv_b[0]),
        _pad_lanes(row(dt_bias[0])), _pad_lanes(row(A_log[0])), row(jnp.repeat(D_skip[0], HEADDIM)),
        row(ssd_norm[0]), sc_conv_w[0].astype(F32), row(sc_norm[0]), w_out[0].astype(BF16), row(norm_ffn[0]),
        wr_hi, wr_lo, br, tril2, e1, e2, ones, bd64,
    ]
    xmap = lambda bi, j: (bi, jnp.maximum(j - 1, 0), 0)
    h1, u2, comb = pl.pallas_call(
        functools.partial(_mixer_kernel, tm=tm),
        grid=(b, nblk + 1),
        in_specs=[pl.BlockSpec((1, tm, d), xmap)] + [_const_spec(c.shape) for c in consts],
        out_specs=[pl.BlockSpec((1, tm, d), xmap), pl.BlockSpec((1, tm, d), xmap),
                   pl.BlockSpec((1, tm, LANES), xmap)],
        out_shape=[jax.ShapeDtypeStruct((b, seq, d), F32), jax.ShapeDtypeStruct((b, seq, d), BF16),
                   jax.ShapeDtypeStruct((b, seq, LANES), F32)],
        scratch_shapes=[
            pltpu.VMEM((tm, W_MAIN_COLS), F32),
            pltpu.VMEM((tm + TAIL, 2048), F32),
            pltpu.VMEM((tm + TAIL, D_SC), F32),
            pltpu.VMEM((tm, D_SSD), F32),
            pltpu.VMEM((tm, GROUPS * STATE), F32),
            pltpu.VMEM((tm, GROUPS * STATE), F32),
            pltpu.VMEM((tm, LANES), F32),
            pltpu.VMEM((tm, D_SSD + D_SC), BF16),
            pltpu.VMEM((GROUPS, STATE, GROUP_W), F32),
        ],
        compiler_params=pltpu.CompilerParams(
            dimension_semantics=("arbitrary", "arbitrary"), vmem_limit_bytes=VMEM_LIMIT_BYTES),
        name="mixer",
    )(x.astype(F32), *consts)

    t = b * seq
    tmo = 1024 if t % 1024 == 0 else tm
    h1f = h1.reshape(t, d)
    u2f = u2.reshape(t, d)
    combf = comb.reshape(t, LANES)
    out = pl.pallas_call(
        functools.partial(_moe_kernel, tmo=tmo),
        grid=(t // tmo, N_EXPERTS),
        in_specs=[
            pl.BlockSpec((tmo, d), lambda i, e: (i, 0)),
            pl.BlockSpec((tmo, d), lambda i, e: (i, 0)),
            pl.BlockSpec((tmo, LANES), lambda i, e: (i, 0)),
            pl.BlockSpec((1, d, D_EXPERT), lambda i, e: (e, 0, 0)),
            pl.BlockSpec((1, d, D_EXPERT), lambda i, e: (e, 0, 0)),
            pl.BlockSpec((1, D_EXPERT, d), lambda i, e: (e, 0, 0)),
            pl.BlockSpec((1, d), lambda i, e: (0, 0)),
        ],
        out_specs=pl.BlockSpec((tmo, d), lambda i, e: (i, 0)),
        out_shape=jax.ShapeDtypeStruct((t, d), F32),
        compiler_params=pltpu.CompilerParams(
            dimension_semantics=("arbitrary", "arbitrary"), vmem_limit_bytes=VMEM_LIMIT_BYTES),
        name="moe",
    )(h1f, u2f, combf, expert_w_gate[0].astype(BF16), expert_w_up[0].astype(BF16),
      expert_w_down[0].astype(BF16), row(norm_final))
    return out.reshape(b, seq, d)
```
